```python
import math
import jax, jax.numpy as jnp
from jax import lax
import numpy as np

D_MODEL = 2048
BATCH = 16
SEQ = 2048
DEPTH = 4

N_BRANCHES = 4
BRANCH_WIDTH = D_MODEL // 4
FOX_HEAD_DIM = 128
FOX_HEADS = BRANCH_WIDTH // FOX_HEAD_DIM
DIFF_V_DIM = 128
DIFF_HEADS = BRANCH_WIDTH // DIFF_V_DIM
DIFF_QK_DIM = DIFF_V_DIM // 2
SSD_HEAD_DIM = 64
SSD_HEADS = BRANCH_WIDTH // SSD_HEAD_DIM
SSD_GROUPS = 2
SSD_STATE = 128
SSD_CONV = 4
SSD_CHUNK = 128
SSD_CONV_DIM = BRANCH_WIDTH + 2 * SSD_GROUPS * SSD_STATE
S5_GROUP_CH = 16
S5_GROUPS = BRANCH_WIDTH // S5_GROUP_CH
S5_STATE = 64

Q_BLOCK = 128
ROPE_THETA = 10000.0
NORM_EPS = 1e-6

SEGMENT_SIZES = (
    BRANCH_WIDTH, BRANCH_WIDTH, BRANCH_WIDTH, FOX_HEADS, BRANCH_WIDTH,
    BRANCH_WIDTH, BRANCH_WIDTH, BRANCH_WIDTH, BRANCH_WIDTH,
    BRANCH_WIDTH, SSD_CONV_DIM, SSD_HEADS,
    BRANCH_WIDTH, BRANCH_WIDTH,
    N_BRANCHES * D_MODEL,
)
IN_COLS = sum(SEGMENT_SIZES)

kernel_name = 'hybrid_fox_diff_ssd_s5_gated_merge'

F32 = jnp.float32


def _rmsnorm(x, g):
    xf = x.astype(F32)
    y = xf * lax.rsqrt(jnp.mean(xf * xf, axis=-1, keepdims=True) + NORM_EPS)
    return (y * g.astype(F32)).astype(x.dtype)


def _combined_projection(xn, w):
    outs = []
    off = 0
    for size in SEGMENT_SIZES:
        outs.append(xn @ w[:, off:off + size])
        off += size
    return outs


def _rope(t, pos):
    d = t.shape[-1]
    half = d // 2
    inv = ROPE_THETA ** (-jnp.arange(half, dtype=F32) / half)
    ang = pos[:, None] * inv[None, :]
    shp = (1, t.shape[1]) + (1,) * (t.ndim - 3) + (half,)
    cos = jnp.cos(ang).reshape(shp)
    sin = jnp.sin(ang).reshape(shp)
    tf = t.astype(F32)
    t1, t2 = tf[..., :half], tf[..., half:]
    return jnp.concatenate([t1 * cos - t2 * sin, t1 * sin + t2 * cos], axis=-1).astype(t.dtype)


def _causal_block_probs(q_blk, k_pre, q_start, bias=None):
    scale = q_blk.shape[-1] ** -0.5
    s = jnp.einsum('bhqd,bhkd->bhqk', q_blk.astype(F32), k_pre.astype(F32)) * scale
    if bias is not None:
        s = s + bias
    qpos = q_start + jnp.arange(q_blk.shape[2])
    kpos = jnp.arange(k_pre.shape[2])
    s = jnp.where(qpos[:, None] >= kpos[None, :], s, -jnp.inf)
    return jax.nn.softmax(s, axis=-1)


def _forgetting_attention(q, k, v, f_logit):
    Bsz, S, H, d = q.shape
    F = jnp.cumsum(jax.nn.log_sigmoid(f_logit.astype(F32)), axis=1).transpose(0, 2, 1)
    qh, kh, vh = (t.transpose(0, 2, 1, 3) for t in (q, k, v))
    outs = []
    for start in range(0, S, Q_BLOCK):
        end = start + Q_BLOCK
        bias = F[:, :, start:end, None] - F[:, :, None, :end]
        p = _causal_block_probs(qh[:, :, start:end], kh[:, :, :end], start, bias)
        outs.append(jnp.einsum('bhqk,bhkd->bhqd', p, vh[:, :, :end].astype(F32)))
    o = jnp.concatenate(outs, axis=2).transpose(0, 2, 1, 3)
    return o.reshape(Bsz, S, H * d).astype(q.dtype)


def _differential_attention(q, k, v, lam, sub_gain, lambda_init):
    Bsz, S, H, _, _ = q.shape
    q1 = q[:, :, :, 0].transpose(0, 2, 1, 3)
    q2 = q[:, :, :, 1].transpose(0, 2, 1, 3)
    k1 = k[:, :, :, 0].transpose(0, 2, 1, 3)
    k2 = k[:, :, :, 1].transpose(0, 2, 1, 3)
    vh = v.transpose(0, 2, 1, 3)
    outs = []
    for start in range(0, S, Q_BLOCK):
        end = start + Q_BLOCK
        p1 = _causal_block_probs(q1[:, :, start:end], k1[:, :, :end], start)
        p2 = _causal_block_probs(q2[:, :, start:end], k2[:, :, :end], start)
        outs.append(jnp.einsum('bhqk,bhkd->bhqd', p1 - lam * p2, vh[:, :, :end].astype(F32)))
    o = jnp.concatenate(outs, axis=2).transpose(0, 2, 1, 3)
    o = _rmsnorm(o, sub_gain) * (1.0 - lambda_init)
    return o.reshape(Bsz, S, H * DIFF_V_DIM).astype(v.dtype)


def _segsum(x):
    T = x.shape[-1]
    xe = jnp.broadcast_to(x[..., None], x.shape + (T,))
    xe = jnp.where(jnp.tril(jnp.ones((T, T), bool), -1), xe, 0.0)
    cs = jnp.cumsum(xe, axis=-2)
    return jnp.where(jnp.tril(jnp.ones((T, T), bool), 0), cs, -jnp.inf)


def _ssd_chunked(X, A, Bh, Ch):
    b, l, h, p = X.shape
    n = Bh.shape[-1]
    c = l // SSD_CHUNK
    X = X.reshape(b, c, SSD_CHUNK, h, p)
    Bh = Bh.reshape(b, c, SSD_CHUNK, h, n)
    Ch = Ch.reshape(b, c, SSD_CHUNK, h, n)
    A = A.reshape(b, c, SSD_CHUNK, h).transpose(0, 3, 1, 2)
    A_cum = jnp.cumsum(A, axis=-1)
    L = jnp.exp(_segsum(A))
    y_diag = jnp.einsum('bclhn,bcshn,bhcls,bcshp->bclhp', Ch, Bh, L, X)
    decay_states = jnp.exp(A_cum[..., -1:] - A_cum)
    states = jnp.einsum('bclhn,bhcl,bclhp->bchpn', Bh, decay_states, X)
    states = jnp.concatenate([jnp.zeros_like(states[:, :1]), states], axis=1)
    chunk_tot = jnp.pad(A_cum[..., -1], ((0, 0), (0, 0), (1, 0)))
    decay_chunk = jnp.exp(_segsum(chunk_tot))
    states = jnp.einsum('bhzc,bchpn->bzhpn', decay_chunk, states)[:, :-1]
    y_off = jnp.einsum('bclhn,bchpn,bhcl->bclhp', Ch, states, jnp.exp(A_cum))
    return (y_diag + y_off).reshape(b, l, h, p)


def _ssd_branch(z, xbc, dt_raw, conv_w, conv_b, dt_bias, a_log, d_skip, norm_g):
    Bsz, S, _ = xbc.shape
    conv = lax.conv_general_dilated(
        xbc.astype(F32), conv_w.astype(F32)[:, None, :], window_strides=(1,),
        padding=[(SSD_CONV - 1, 0)], dimension_numbers=('NWC', 'WIO', 'NWC'),
        feature_group_count=SSD_CONV_DIM)
    xbc = jax.nn.silu(conv + conv_b.astype(F32))
    gn = SSD_GROUPS * SSD_STATE
    xs = xbc[..., :BRANCH_WIDTH].reshape(Bsz, S, SSD_HEADS, SSD_HEAD_DIM)
    rep = SSD_HEADS // SSD_GROUPS
    bm = jnp.repeat(xbc[..., BRANCH_WIDTH:BRANCH_WIDTH + gn].reshape(Bsz, S, SSD_GROUPS, SSD_STATE), rep, axis=2)
    cm = jnp.repeat(xbc[..., BRANCH_WIDTH + gn:].reshape(Bsz, S, SSD_GROUPS, SSD_STATE), rep, axis=2)
    dt = jax.nn.softplus(dt_raw.astype(F32) + dt_bias.astype(F32))
    a = -jnp.exp(a_log.astype(F32))
    y = _ssd_chunked(xs * dt[..., None], a * dt, bm, cm)
    y = y + d_skip.astype(F32)[:, None] * xs
    y = y.reshape(Bsz, S, BRANCH_WIDTH) * jax.nn.silu(z.astype(F32))
    return _rmsnorm(y, norm_g).astype(z.dtype)


def _s5_combine(e1, e2):
    a1, b1 = e1
    a2, b2 = e2
    return a1 * a2, a2 * b1 + b2


def _s5_branch(u, a_re, a_im, b_re, b_im, c_re, c_im, d_skip, log_dt, w_glu):
    Bsz, S, _ = u.shape
    uf = u.astype(F32).reshape(Bsz, S, S5_GROUPS, S5_GROUP_CH)
    lam = lax.complex(a_re.astype(F32), a_im.astype(F32))
    dt = jnp.exp(log_dt.astype(F32))[:, None]
    a_bar = jnp.exp(lam * dt)
    b_bar = ((a_bar - 1.0) / lam)[..., None] * lax.complex(b_re.astype(F32), b_im.astype(F32))
    bu = lax.complex(jnp.einsum('gph,blgh->blgp', jnp.real(b_bar), uf),
                     jnp.einsum('gph,blgh->blgp', jnp.imag(b_bar), uf))
    a_seq = jnp.broadcast_to(a_bar, (1, S) + a_bar.shape)
    _, states = lax.associative_scan(_s5_combine, (a_seq, bu), axis=1)
    y = (jnp.einsum('ghp,blgp->blgh', c_re.astype(F32), jnp.real(states))
         - jnp.einsum('ghp,blgp->blgh', c_im.astype(F32), jnp.imag(states))
         + d_skip.astype(F32) * uf)
    y = jax.nn.gelu(y.reshape(Bsz, S, BRANCH_WIDTH))
    ga = y @ w_glu.astype(F32)
    out = ga[..., :BRANCH_WIDTH] * jax.nn.sigmoid(ga[..., BRANCH_WIDTH:])
    return out.astype(u.dtype)


def setup_inputs(seed: int = 0) -> dict:
    key = jax.random.key(seed)
    ks = jax.random.split(key, 24)
    nrm = jax.random.normal
    L, D, W = DEPTH, D_MODEL, BRANCH_WIDTH
    x = nrm(ks[0], (BATCH, SEQ, D), F32)
    w_in = nrm(ks[1], (L, D, IN_COLS), F32) * D ** -0.5
    b_fox_f = jnp.linspace(1.0, 4.0, FOX_HEADS, dtype=F32)[None, :] + 0.1 * nrm(ks[2], (L, FOX_HEADS), F32)
    pre_norm_g = 1.0 + 0.02 * nrm(ks[3], (L, D), F32)
    post_norm_g = 1.0 + 0.02 * nrm(ks[4], (L, D), F32)
    diff_lambda = 0.1 * nrm(ks[5], (L, 4, DIFF_QK_DIM), F32)
    diff_subln_g = 1.0 + 0.02 * nrm(ks[6], (L, DIFF_V_DIM), F32)
    ssd_conv_w = nrm(ks[7], (L, SSD_CONV, SSD_CONV_DIM), F32) * SSD_CONV ** -0.5
    ssd_conv_b = 0.02 * nrm(ks[8], (L, SSD_CONV_DIM), F32)
    dt0 = jnp.exp(jax.random.uniform(ks[9], (L, SSD_HEADS), F32, math.log(1e-3), math.log(1e-1)))
    ssd_dt_bias = dt0 + jnp.log(-jnp.expm1(-dt0))
    ssd_a_log = jnp.log(jax.random.uniform(ks[10], (L, SSD_HEADS), F32, 1.0, 16.0))
    ssd_d = 1.0 + 0.02 * nrm(ks[11], (L, SSD_HEADS), F32)
    ssd_norm_g = 1.0 + 0.02 * nrm(ks[12], (L, W), F32)
    s5_a_re = -0.5 + 0.01 * nrm(ks[13], (L, S5_GROUPS, S5_STATE), F32)
    s5_a_im = (jnp.pi * jnp.arange(S5_STATE, dtype=F32))[None, None, :] + 0.01 * nrm(ks[14], (L, S5_GROUPS, S5_STATE), F32)
    s5_b_re = nrm(ks[15], (L, S5_GROUPS, S5_STATE, S5_GROUP_CH), F32) * (2 * S5_GROUP_CH) ** -0.5
    s5_b_im = nrm(ks[16], (L, S5_GROUPS, S5_STATE, S5_GROUP_CH), F32) * (2 * S5_GROUP_CH) ** -0.5
    s5_c_re = nrm(ks[17], (L, S5_GROUPS, S5_GROUP_CH, S5_STATE), F32) * (2 * S5_STATE) ** -0.5
    s5_c_im = nrm(ks[18], (L, S5_GROUPS, S5_GROUP_CH, S5_STATE), F32) * (2 * S5_STATE) ** -0.5
    s5_d = nrm(ks[19], (L, S5_GROUPS, S5_GROUP_CH), F32)
    s5_log_dt = jax.random.uniform(ks[20], (L, S5_GROUPS), F32, math.log(1e-3), math.log(1e-1))
    s5_w_glu = nrm(ks[21], (L, W, 2 * W), F32) * W ** -0.5
    w_branch = nrm(ks[22], (L, N_BRANCHES, W, D), F32) * W ** -0.5
    w_out = nrm(ks[23], (L, D, D), F32) * D ** -0.5
    return {'x': x, 'w_in': w_in, 'b_fox_f': b_fox_f, 'pre_norm_g': pre_norm_g,
            'post_norm_g': post_norm_g, 'diff_lambda': diff_lambda, 'diff_subln_g': diff_subln_g,
            'ssd_conv_w': ssd_conv_w, 'ssd_conv_b': ssd_conv_b, 'ssd_dt_bias': ssd_dt_bias,
            'ssd_a_log': ssd_a_log, 'ssd_d': ssd_d, 'ssd_norm_g': ssd_norm_g,
            's5_a_re': s5_a_re, 's5_a_im': s5_a_im, 's5_b_re': s5_b_re, 's5_b_im': s5_b_im,
            's5_c_re': s5_c_re, 's5_c_im': s5_c_im, 's5_d': s5_d, 's5_log_dt': s5_log_dt,
            's5_w_glu': s5_w_glu, 'w_branch': w_branch, 'w_out': w_out}


def reference(x, w_in, b_fox_f, pre_norm_g, post_norm_g, diff_lambda, diff_subln_g,
              ssd_conv_w, ssd_conv_b, ssd_dt_bias, ssd_a_log, ssd_d, ssd_norm_g,
              s5_a_re, s5_a_im, s5_b_re, s5_b_im, s5_c_re, s5_c_im, s5_d, s5_log_dt,
              s5_w_glu, w_branch, w_out):
    Bsz, S, _ = x.shape
    pos = jnp.arange(S, dtype=F32)
    h = x
    for l in range(DEPTH):
        xn = _rmsnorm(h, pre_norm_g[l])
        (fq, fk, fv, ff, fg, dq, dk, dv, dg, sz, sxbc, sdt, su, sg, mg) = _combined_projection(xn, w_in[l])

        oa = _forgetting_attention(
            fq.reshape(Bsz, S, FOX_HEADS, FOX_HEAD_DIM), fk.reshape(Bsz, S, FOX_HEADS, FOX_HEAD_DIM),
            fv.reshape(Bsz, S, FOX_HEADS, FOX_HEAD_DIM), ff + b_fox_f[l]) * jax.nn.silu(fg)

        lambda_init = 0.8 - 0.6 * math.exp(-0.3 * l)
        lp = diff_lambda[l].astype(F32)
        lam = jnp.exp(jnp.sum(lp[0] * lp[1])) - jnp.exp(jnp.sum(lp[2] * lp[3])) + lambda_init
        dq5 = _rope(dq.reshape(Bsz, S, DIFF_HEADS, 2, DIFF_QK_DIM), pos)
        dk5 = _rope(dk.reshape(Bsz, S, DIFF_HEADS, 2, DIFF_QK_DIM), pos)
        ob = _differential_attention(dq5, dk5, dv.reshape(Bsz, S, DIFF_HEADS, DIFF_V_DIM),
                                     lam, diff_subln_g[l], lambda_init) * jax.nn.silu(dg)

        oc = _ssd_branch(sz, sxbc, sdt, ssd_conv_w[l], ssd_conv_b[l], ssd_dt_bias[l],
                         ssd_a_log[l], ssd_d[l], ssd_norm_g[l])

        od = _s5_branch(su, s5_a_re[l], s5_a_im[l], s5_b_re[l], s5_b_im[l], s5_c_re[l],
                        s5_c_im[l], s5_d[l], s5_log_dt[l], s5_w_glu[l]) * jax.nn.silu(sg)

        merged = None
        for n, o in enumerate((oa, ob, oc, od)):
            gate = jax.nn.sigmoid(mg[..., n * D_MODEL:(n + 1) * D_MODEL])
            term = gate * (o.astype(h.dtype) @ w_branch[l, n])
            merged = term if merged is None else merged + term
        y = merged @ w_out[l]
        h = h + _rmsnorm(y, post_norm_g[l])
    return h
```

```python
import functools
import math

import jax
import jax.numpy as jnp
from jax import lax
from jax.experimental import pallas as pl
from jax.experimental.pallas import tpu as pltpu

F32 = jnp.float32
BF16 = jnp.bfloat16

D_MODEL = 2048
WIDTH = D_MODEL // 4
HEAD_DIM = 128
N_HEADS = WIDTH // HEAD_DIM
DIFF_QK = HEAD_DIM // 2
SSD_HEADS = 8
SSD_P = WIDTH // SSD_HEADS
SSD_GROUPS = 2
SSD_N = 128
SSD_CONV = 4
SSD_T = 128
SSD_CONV_DIM = WIDTH + 2 * SSD_GROUPS * SSD_N
S5_GROUP_CH = 16
S5_GROUPS = WIDTH // S5_GROUP_CH
S5_P = 64
S5_SLABS = WIDTH // 128
S5_SLAB_STATE = 8 * S5_P
ROPE_THETA = 10000.0
NORM_EPS = 1e-6
LANES = 128
SUBLANES = 8

COL_FQ, COL_FK, COL_FV, COL_FG = 0, 4, 8, 12
COL_DQ, COL_DK, COL_DV, COL_DG = 16, 20, 24, 28
COL_XBC = 32
COL_SZ, COL_SU, COL_SG = 40, 44, 48
NP_COLS = 52 * LANES
PROJ_TN = 512
ROPE_TILES = (COL_DQ * LANES // PROJ_TN, COL_DK * LANES // PROJ_TN)
MISC_FF, MISC_DT = 0, 4

VMEM_LIMIT = 56 * 1024 * 1024


def _cparams(sem):
    return pltpu.CompilerParams(dimension_semantics=sem, vmem_limit_bytes=VMEM_LIMIT)


def _split3(x):
    hi = x.astype(BF16)
    r = x - hi.astype(F32)
    mid = r.astype(BF16)
    lo = (r - mid.astype(F32)).astype(BF16)
    return hi, mid, lo


def _dot(a, b):
    return jnp.dot(a, b, preferred_element_type=F32)


def _dot_nt(a, b):
    return lax.dot_general(a, b, (((1,), (1,)), ((), ())), preferred_element_type=F32)


def _dot3_left_exact(m_bf16, x):
    hi, mid, lo = _split3(x)
    return _dot(m_bf16, hi) + _dot(m_bf16, mid) + _dot(m_bf16, lo)


def _dot3_right_exact(x, m_bf16):
    hi, mid, lo = _split3(x)
    return _dot(hi, m_bf16) + _dot(mid, m_bf16) + _dot(lo, m_bf16)


def _sigmoid(x):
    return 1.0 / (1.0 + jnp.exp(-x))


def _silu(x):
    return x * _sigmoid(x)


def _softplus(x):
    return jnp.maximum(x, 0.0) + jnp.log(1.0 + jnp.exp(-jnp.abs(x)))


def _rmsnorm_kernel(x_ref, g_ref, o_ref):
    x = x_ref[...]
    y = x * lax.rsqrt(jnp.mean(x * x, axis=-1, keepdims=True) + NORM_EPS)
    o_ref[...] = (y * g_ref[...]).astype(BF16)


def _rmsnorm(x2d, g):
    m = x2d.shape[0]
    tm = 512
    return pl.pallas_call(
        _rmsnorm_kernel,
        grid=(m // tm,),
        in_specs=[pl.BlockSpec((tm, D_MODEL), lambda i: (i, 0)),
                  pl.BlockSpec((1, D_MODEL), lambda i: (0, 0))],
        out_specs=pl.BlockSpec((tm, D_MODEL), lambda i: (i, 0)),
        out_shape=jax.ShapeDtypeStruct((m, D_MODEL), BF16),
        compiler_params=_cparams(("parallel",)),
        name="rmsnorm",
    )(x2d, g.reshape(1, D_MODEL))


def _in_proj_kernel(xn_ref, w_ref, wm_ref, cos_ref, sin_ref, p_ref, misc_ref):
    j = pl.program_id(1)
    acc = _dot(xn_ref[...], w_ref[...])
    is_rope = (j == ROPE_TILES[0]) | (j == ROPE_TILES[1])

    @pl.when(is_rope)
    def _():
        cos = cos_ref[...]
        sin = sin_ref[...]
        lane = lax.broadcasted_iota(jnp.int32, (1, LANES), 1)
        first_half = (lane % DIFF_QK) < (DIFF_QK // 2)
        for c in range(PROJ_TN // LANES):
            t = acc[:, c * LANES:(c + 1) * LANES]
            up = pltpu.roll(t, LANES - DIFF_QK // 2, 1)
            dn = pltpu.roll(t, DIFF_QK // 2, 1)
            partner = jnp.where(first_half, up, dn)
            p_ref[:, c * LANES:(c + 1) * LANES] = (t * cos + partner * sin).astype(BF16)

    @pl.when(jnp.logical_not(is_rope))
    def _():
        p_ref[...] = acc.astype(BF16)

    @pl.when(j == 0)
    def _():
        misc_ref[...] = _dot(xn_ref[...], wm_ref[...])


def _in_proj(xn, w, wm, cos_t, sin_t, seq):
    m = xn.shape[0]
    tm = min(1024, seq)
    nseq = seq // tm
    return pl.pallas_call(
        _in_proj_kernel,
        grid=(m // tm, NP_COLS // PROJ_TN),
        in_specs=[pl.BlockSpec((tm, D_MODEL), lambda i, j: (i, 0)),
                  pl.BlockSpec((D_MODEL, PROJ_TN), lambda i, j: (0, j)),
                  pl.BlockSpec((D_MODEL, LANES), lambda i, j: (0, 0)),
                  pl.BlockSpec((tm, LANES), lambda i, j: (i % nseq, 0)),
                  pl.BlockSpec((tm, LANES), lambda i, j: (i % nseq, 0))],
        out_specs=[pl.BlockSpec((tm, PROJ_TN), lambda i, j: (i, j)),
                   pl.BlockSpec((tm, LANES), lambda i, j: (i, 0))],
        out_shape=[jax.ShapeDtypeStruct((m, NP_COLS), BF16),
                   jax.ShapeDtypeStruct((m, LANES), F32)],
        compiler_params=_cparams(("parallel", "arbitrary")),
        name="in_proj",
    )(xn, w, wm, cos_t, sin_t)


def _fox_prep_kernel(misc_ref, bias_ref, fcol_ref, frow_ref, *, seq):
    row = lax.broadcasted_iota(jnp.int32, (SSD_T, SSD_T), 0)
    col = lax.broadcasted_iota(jnp.int32, (SSD_T, SSD_T), 1)
    tri = jnp.where(row >= col, 1.0, 0.0).astype(BF16)
    carry = jnp.zeros((1, LANES), F32)
    for c in range(seq // SSD_T):
        x = misc_ref[c * SSD_T:(c + 1) * SSD_T, :] + bias_ref[...]
        ls = jnp.minimum(x, 0.0) - jnp.log(1.0 + jnp.exp(-jnp.abs(x)))
        f = _dot3_left_exact(tri, ls) + carry
        fcol_ref[c * SSD_T:(c + 1) * SSD_T, :] = f
        frow_ref[0, :, c * SSD_T:(c + 1) * SSD_T] = f.T[0:SUBLANES, :]
        carry = f[SSD_T - 1:SSD_T, :]


def _fox_prep(misc, bias, batch, seq):
    return pl.pallas_call(
        functools.partial(_fox_prep_kernel, seq=seq),
        grid=(batch,),
        in_specs=[pl.BlockSpec((seq, LANES), lambda b: (b, 0)),
                  pl.BlockSpec((1, LANES), lambda b: (0, 0))],
        out_specs=[pl.BlockSpec((seq, LANES), lambda b: (b, 0)),
                   pl.BlockSpec((1, SUBLANES, seq), lambda b: (b, 0, 0))],
        out_shape=[jax.ShapeDtypeStruct((batch * seq, LANES), F32),
                   jax.ShapeDtypeStruct((batch, SUBLANES, seq), F32)],
        compiler_params=_cparams(("parallel",)),
        name="fox_prep",
    )(misc, bias)


def _fox_attn_kernel(q_ref, k_ref, v_ref, g_ref, fcol_ref, frow_ref, o_ref,
                     m_ref, l_ref, acc_ref, *, tq):
    h = pl.program_id(1)
    qi = pl.program_id(2)
    scale = HEAD_DIM ** -0.5
    q = (q_ref[...].astype(F32) * scale).astype(BF16)
    lane = lax.broadcasted_iota(jnp.int32, (1, LANES), 1)
    fq = jnp.sum(jnp.where(lane == h, fcol_ref[...], 0.0), axis=1, keepdims=True)
    m_ref[...] = jnp.full((tq, 1), -jnp.inf, F32)
    l_ref[...] = jnp.zeros((tq, 1), F32)
    acc_ref[...] = jnp.zeros((tq, HEAD_DIM), F32)

    def block(ki, masked):
        start = pl.multiple_of(ki * tq, tq)
        k = k_ref[pl.ds(start, tq), :]
        v = v_ref[pl.ds(start, tq), :]
        fk = frow_ref[0, pl.ds(h, 1), pl.ds(start, tq)]
        s = _dot_nt(q, k) + fq - fk
        if masked:
            r = lax.broadcasted_iota(jnp.int32, (tq, tq), 0)
            c = lax.broadcasted_iota(jnp.int32, (tq, tq), 1)
            s = jnp.where(r >= c, s, -jnp.inf)
        m_old = m_ref[...]
        m_new = jnp.maximum(m_old, jnp.max(s, axis=1, keepdims=True))
        p = jnp.exp(s - m_new)
        alpha = jnp.exp(m_old - m_new)
        l_ref[...] = alpha * l_ref[...] + jnp.sum(p, axis=1, keepdims=True)
        acc_ref[...] = alpha * acc_ref[...] + _dot(p.astype(BF16), v)
        m_ref[...] = m_new

    block(qi, True)

    def body(ki, carry):
        block(ki, False)
        return carry

    lax.fori_loop(0, qi, body, 0)
    g = g_ref[...].astype(F32)
    o_ref[...] = (acc_ref[...] / l_ref[...] * _silu(g)).astype(BF16)


def _fox_attn(p, fcol, frow, batch, seq):
    tq = min(256, seq)
    nq = seq // tq
    m = batch * seq
    return pl.pallas_call(
        functools.partial(_fox_attn_kernel, tq=tq),
        grid=(batch, N_HEADS, nq),
        in_specs=[pl.BlockSpec((tq, LANES), lambda b, h, i: (b * nq + i, COL_FQ + h)),
                  pl.BlockSpec((seq, LANES), lambda b, h, i: (b, COL_FK + h)),
                  pl.BlockSpec((seq, LANES), lambda b, h, i: (b, COL_FV + h)),
                  pl.BlockSpec((tq, LANES), lambda b, h, i: (b * nq + i, COL_FG + h)),
                  pl.BlockSpec((tq, LANES), lambda b, h, i: (b * nq + i, 0)),
                  pl.BlockSpec((1, SUBLANES, seq), lambda b, h, i: (b, 0, 0))],
        out_specs=pl.BlockSpec((tq, LANES), lambda b, h, i: (b * nq + i, h)),
        out_shape=jax.ShapeDtypeStruct((m, WIDTH), BF16),
        scratch_shapes=[pltpu.VMEM((tq, 1), F32), pltpu.VMEM((tq, 1), F32),
                        pltpu.VMEM((tq, HEAD_DIM), F32)],
        compiler_params=_cparams(("parallel", "parallel", "arbitrary")),
        name="fox_attn",
    )(p, p, p, p, fcol, frow)


def _diff_attn_kernel(q_ref, k_ref, v_ref, g_ref, lam_ref, gain_ref, o_ref,
                      m_ref, l_ref, acc_ref, *, tq, lambda_init):
    qi = pl.program_id(2)
    scale = DIFF_QK ** -0.5
    lane = lax.broadcasted_iota(jnp.int32, (1, LANES), 1)
    qf = q_ref[...].astype(F32) * scale
    qs = (jnp.where(lane < DIFF_QK, qf, 0.0).astype(BF16),
          jnp.where(lane >= DIFF_QK, qf, 0.0).astype(BF16))
    m_ref[...] = jnp.full((2, tq, 1), -jnp.inf, F32)
    l_ref[...] = jnp.zeros((2, tq, 1), F32)
    acc_ref[...] = jnp.zeros((2, tq, HEAD_DIM), F32)

    def block(ki, masked):
        start = pl.multiple_of(ki * tq, tq)
        k = k_ref[pl.ds(start, tq), :]
        v = v_ref[pl.ds(start, tq), :]
        if masked:
            r = lax.broadcasted_iota(jnp.int32, (tq, tq), 0)
            c = lax.broadcasted_iota(jnp.int32, (tq, tq), 1)
            keep = r >= c
        for a in range(2):
            s = _dot_nt(qs[a], k)
            if masked:
                s = jnp.where(keep, s, -jnp.inf)
            m_old = m_ref[a]
            m_new = jnp.maximum(m_old, jnp.max(s, axis=1, keepdims=True))
            p = jnp.exp(s - m_new)
            alpha = jnp.exp(m_old - m_new)
            l_ref[a] = alpha * l_ref[a] + jnp.sum(p, axis=1, keepdims=True)
            acc_ref[a] = alpha * acc_ref[a] + _dot(p.astype(BF16), v)
            m_ref[a] = m_new

    block(qi, True)

    def body(ki, carry):
        block(ki, False)
        return carry

    lax.fori_loop(0, qi, body, 0)
    lp = lam_ref[...]
    lam = (jnp.exp(jnp.sum(lp[0:1] * lp[1:2], axis=1, keepdims=True))
           - jnp.exp(jnp.sum(lp[2:3] * lp[3:4], axis=1, keepdims=True)) + lambda_init)
    o = acc_ref[0] / l_ref[0] - lam * (acc_ref[1] / l_ref[1])
    o = o * lax.rsqrt(jnp.mean(o * o, axis=1, keepdims=True) + NORM_EPS) * gain_ref[...]
    o = o * (1.0 - lambda_init)
    g = g_ref[...].astype(F32)
    o_ref[...] = (o * _silu(g)).astype(BF16)


def _diff_attn(p, lam_p, gain, lambda_init, batch, seq):
    tq = min(256, seq)
    nq = seq // tq
    m = batch * seq
    return pl.pallas_call(
        functools.partial(_diff_attn_kernel, tq=tq, lambda_init=lambda_init),
        grid=(batch, N_HEADS, nq),
        in_specs=[pl.BlockSpec((tq, LANES), lambda b, h, i: (b * nq + i, COL_DQ + h)),
                  pl.BlockSpec((seq, LANES), lambda b, h, i: (b, COL_DK + h)),
                  pl.BlockSpec((seq, LANES), lambda b, h, i: (b, COL_DV + h)),
                  pl.BlockSpec((tq, LANES), lambda b, h, i: (b * nq + i, COL_DG + h)),
                  pl.BlockSpec((4, DIFF_QK), lambda b, h, i: (0, 0)),
                  pl.BlockSpec((1, HEAD_DIM), lambda b, h, i: (0, 0))],
        out_specs=pl.BlockSpec((tq, LANES), lambda b, h, i: (b * nq + i, h)),
        out_shape=jax.ShapeDtypeStruct((m, WIDTH), BF16),
        scratch_shapes=[pltpu.VMEM((2, tq, 1), F32), pltpu.VMEM((2, tq, 1), F32),
                        pltpu.VMEM((2, tq, HEAD_DIM), F32)],
        compiler_params=_cparams(("parallel", "parallel", "arbitrary")),
        name="diff_attn",
    )(p, p, p, p, lam_p, gain.reshape(1, HEAD_DIM))


def _ssd_kernel(xbc_ref, z_ref, misc_ref, cw_ref, cb_ref, dtb_ref, alog_ref, dsk_ref, ng_ref,
                e512_ref, e1024_ref, o_ref, xpad_ref, state_ref):
    c = pl.program_id(1)
    T = SSD_T

    @pl.when(c == 0)
    def _():
        xpad_ref[0:SUBLANES, :] = jnp.zeros((SUBLANES, SSD_CONV_DIM), F32)
        state_ref[...] = jnp.zeros((SSD_N, WIDTH), F32)

    cur = xbc_ref[...].astype(F32)
    xpad_ref[SUBLANES:SUBLANES + T, :] = cur
    conv = cb_ref[...] + cw_ref[SSD_CONV - 1:SSD_CONV, :] * cur
    for kk in range(SSD_CONV - 1):
        off = SUBLANES - (SSD_CONV - 1) + kk
        conv = conv + cw_ref[kk:kk + 1, :] * xpad_ref[off:off + T, :]
    xpad_ref[0:SUBLANES, :] = cur[T - SUBLANES:T, :]
    xc = _silu(conv)
    xs = xc[:, 0:WIDTH]
    gn = SSD_GROUPS * SSD_N
    bm = xc[:, WIDTH:WIDTH + gn]
    cm = xc[:, WIDTH + gn:WIDTH + 2 * gn]

    dt = _softplus(misc_ref[...] + dtb_ref[...])
    a = -jnp.exp(alog_ref[...])
    d_a = dt * a
    row = lax.broadcasted_iota(jnp.int32, (T, T), 0)
    col = lax.broadcasted_iota(jnp.int32, (T, T), 1)
    causal = row >= col
    tri = jnp.where(causal, 1.0, 0.0).astype(BF16)
    acum = _dot3_left_exact(tri, d_a)
    acum_t = acum.T
    e512 = e512_ref[...]
    acum_full = _dot3_right_exact(acum, e512)
    dt_full = _dot3_right_exact(dt, e512)
    acol = _dot3_right_exact(acum, e1024_ref[...])
    alast_full = acum_full[T - 1:T, :]
    exp_a = jnp.exp(acum_full)
    decay = jnp.exp(alast_full - acum_full)
    xdt = xs * dt_full
    xdt_b = xdt.astype(BF16)
    xdec_b = (xdt * decay).astype(BF16)
    lane = lax.broadcasted_iota(jnp.int32, (1, LANES), 1)
    lo_half = lane < SSD_P

    state = state_ref[...]
    heads_per_group = SSD_HEADS // SSD_GROUPS
    y_parts = []
    new_state_parts = []
    for g in range(SSD_GROUPS):
        bg = bm[:, g * SSD_N:(g + 1) * SSD_N]
        cg_b = cm[:, g * SSD_N:(g + 1) * SSD_N].astype(BF16)
        gmat = _dot_nt(cg_b, bg.astype(BF16))
        gw = heads_per_group * SSD_P
        y_off = _dot(cg_b, state[:, g * gw:(g + 1) * gw].astype(BF16)) * exp_a[:, g * gw:(g + 1) * gw]
        new_state_parts.append(_dot(bg.T.astype(BF16), xdec_b[:, g * gw:(g + 1) * gw]))
        for pr in range(heads_per_group // 2):
            pair = g * (heads_per_group // 2) + pr
            xp = xdt_b[:, pair * LANES:(pair + 1) * LANES]
            y_pair = y_off[:, pr * LANES:(pr + 1) * LANES]
            for sub in range(2):
                hh = 2 * pair + sub
                diff = acol[:, hh * LANES:(hh + 1) * LANES] - acum_t[MISC_DT + hh:MISC_DT + hh + 1, :]
                lmat = jnp.where(causal, jnp.exp(diff), 0.0)
                mh = (gmat * lmat).astype(BF16)
                xh = jnp.where(lo_half if sub == 0 else jnp.logical_not(lo_half), xp, jnp.zeros_like(xp))
                y_pair = y_pair + _dot(mh, xh)
            y_parts.append(y_pair)
    y = jnp.concatenate(y_parts, axis=1)
    state_ref[...] = state * jnp.exp(alast_full) + jnp.concatenate(new_state_parts, axis=1)

    y = y + dsk_ref[...] * xs
    y = y * _silu(z_ref[...].astype(F32))
    y = y * lax.rsqrt(jnp.mean(y * y, axis=1, keepdims=True) + NORM_EPS) * ng_ref[...]
    o_ref[...] = y.astype(BF16)


def _ssd(p, misc, cw, cb, dtb, alog, dsk, ng, e512, e1024, batch, seq):
    nc = seq // SSD_T
    m = batch * seq
    full = lambda shape: pl.BlockSpec(shape, lambda b, c: (0, 0))
    return pl.pallas_call(
        _ssd_kernel,
        grid=(batch, nc),
        in_specs=[pl.BlockSpec((SSD_T, SSD_CONV_DIM), lambda b, c: (b * nc + c, COL_XBC * LANES // SSD_CONV_DIM)),
                  pl.BlockSpec((SSD_T, WIDTH), lambda b, c: (b * nc + c, COL_SZ * LANES // WIDTH)),
                  pl.BlockSpec((SSD_T, LANES), lambda b, c: (b * nc + c, 0)),
                  full((SSD_CONV, SSD_CONV_DIM)), full((1, SSD_CONV_DIM)),
                  full((1, LANES)), full((1, LANES)), full((1, WIDTH)), full((1, WIDTH)),
                  full((LANES, WIDTH)), full((LANES, SSD_HEADS * LANES))],
        out_specs=pl.BlockSpec((SSD_T, WIDTH), lambda b, c: (b * nc + c, 0)),
        out_shape=jax.ShapeDtypeStruct((m, WIDTH), BF16),
        scratch_shapes=[pltpu.VMEM((SUBLANES + SSD_T, SSD_CONV_DIM), F32),
                        pltpu.VMEM((SSD_N, WIDTH), F32)],
        compiler_params=_cparams(("parallel", "arbitrary")),
        name="ssd",
    )(p, p, misc, cw, cb, dtb, alog, dsk, ng, e512, e1024)


def _gelu_tanh(x):
    return 0.5 * x * (1.0 + jnp.tanh(math.sqrt(2.0 / math.pi) * (x + 0.044715 * (x * x * x))))


def _s5_kernel(u_ref, bmat_ref, cmat_ref, are_ref, aim_ref, d_ref, o_ref,
               utb_ref, st_ref, carry_ref, ytb_ref, *, tc):
    t_idx = pl.program_id(2)
    nb = SUBLANES
    ns = S5_SLAB_STATE

    @pl.when(t_idx == 0)
    def _():
        carry_ref[...] = jnp.zeros((nb, 2 * ns), F32)

    for b in range(nb):
        utb_ref[pl.ds(b, tc, stride=nb), :] = u_ref[b].astype(F32)
    utb = utb_ref[...]
    st_ref[...] = _dot(utb.astype(BF16), bmat_ref[0])

    are = jnp.broadcast_to(are_ref[0], (nb, ns))
    aim = jnp.broadcast_to(aim_ref[0], (nb, ns))

    def step(t, carry):
        sre, sim = carry
        r = pl.multiple_of(t * nb, nb)
        bre = st_ref[pl.ds(r, nb), 0:ns]
        bim = st_ref[pl.ds(r, nb), ns:2 * ns]
        nre = are * sre - aim * sim + bre
        nim = are * sim + aim * sre + bim
        st_ref[pl.ds(r, nb), 0:ns] = nre
        st_ref[pl.ds(r, nb), ns:2 * ns] = nim
        return nre, nim

    sre, sim = lax.fori_loop(0, tc, step, (carry_ref[:, 0:ns], carry_ref[:, ns:2 * ns]), unroll=8)
    carry_ref[:, 0:ns] = sre
    carry_ref[:, ns:2 * ns] = sim

    y = _dot(st_ref[...].astype(BF16), cmat_ref[0]) + d_ref[0] * utb
    ytb_ref[...] = _gelu_tanh(y)
    for b in range(nb):
        o_ref[b] = ytb_ref[pl.ds(b, tc, stride=nb), :].astype(BF16)


def _s5(p3, bmat, cmat, are, aim, dvec, batch, seq):
    tc = min(256, seq)
    nb = SUBLANES
    return pl.pallas_call(
        functools.partial(_s5_kernel, tc=tc),
        grid=(batch // nb, S5_SLABS, seq // tc),
        in_specs=[pl.BlockSpec((nb, tc, LANES), lambda g, s, t: (g, t, COL_SU + s)),
                  pl.BlockSpec((1, LANES, 2 * S5_SLAB_STATE), lambda g, s, t: (s, 0, 0)),
                  pl.BlockSpec((1, 2 * S5_SLAB_STATE, LANES), lambda g, s, t: (s, 0, 0)),
                  pl.BlockSpec((1, 1, S5_SLAB_STATE), lambda g, s, t: (s, 0, 0)),
                  pl.BlockSpec((1, 1, S5_SLAB_STATE), lambda g, s, t: (s, 0, 0)),
                  pl.BlockSpec((1, 1, LANES), lambda g, s, t: (s, 0, 0))],
        out_specs=pl.BlockSpec((nb, tc, LANES), lambda g, s, t: (g, t, s)),
        out_shape=jax.ShapeDtypeStruct((batch, seq, WIDTH), BF16),
        scratch_shapes=[pltpu.VMEM((nb * tc, LANES), F32),
                        pltpu.VMEM((nb * tc, 2 * S5_SLAB_STATE), F32),
                        pltpu.VMEM((nb, 2 * S5_SLAB_STATE), F32),
                        pltpu.VMEM((nb * tc, LANES), F32)],
        compiler_params=_cparams(("parallel", "parallel", "arbitrary")),
        name="s5",
    )(p3, bmat, cmat, are, aim, dvec)


def _merge_kernel(xn_ref, oa_ref, ob_ref, oc_ref, y5_ref, sg_ref, wglu_ref,
                  wg0_ref, wg1_ref, wg2_ref, wg3_ref, wb_ref, o_ref, od_ref):
    j = pl.program_id(1)

    @pl.when(j == 0)
    def _():
        ga = _dot(y5_ref[...], wglu_ref[...])
        od = ga[:, 0:WIDTH] * _sigmoid(ga[:, WIDTH:2 * WIDTH]) * _silu(sg_ref[...].astype(F32))
        od_ref[...] = od.astype(BF16)

    xn = xn_ref[...]
    merged = None
    for n, (o_n, wg) in enumerate(((oa_ref, wg0_ref), (ob_ref, wg1_ref), (oc_ref, wg2_ref), (od_ref, wg3_ref))):
        term = _sigmoid(_dot(xn, wg[...])) * _dot(o_n[...], wb_ref[n])
        merged = term if merged is None else merged + term
    o_ref[...] = merged.astype(BF16)


def _merge(xn, oa, ob, oc, y5, p, wglu, wmg, wb, seq):
    m = xn.shape[0]
    tm = min(1024, seq)
    tn = 256
    nj = D_MODEL // tn
    row = lambda i, j: (i, 0)
    gate_spec = lambda n: pl.BlockSpec((D_MODEL, tn), lambda i, j: (0, n * nj + j))
    return pl.pallas_call(
        _merge_kernel,
        grid=(m // tm, nj),
        in_specs=[pl.BlockSpec((tm, D_MODEL), row),
                  pl.BlockSpec((tm, WIDTH), row), pl.BlockSpec((tm, WIDTH), row),
                  pl.BlockSpec((tm, WIDTH), row), pl.BlockSpec((tm, WIDTH), row),
                  pl.BlockSpec((tm, WIDTH), lambda i, j: (i, COL_SG * LANES // WIDTH)),
                  pl.BlockSpec((WIDTH, 2 * WIDTH), lambda i, j: (0, 0)),
                  gate_spec(0), gate_spec(1), gate_spec(2), gate_spec(3),
                  pl.BlockSpec((4, WIDTH, tn), lambda i, j: (0, 0, j))],
        out_specs=pl.BlockSpec((tm, tn), lambda i, j: (i, j)),
        out_shape=jax.ShapeDtypeStruct((m, D_MODEL), BF16),
        scratch_shapes=[pltpu.VMEM((tm, WIDTH), BF16)],
        compiler_params=_cparams(("parallel", "arbitrary")),
        name="merge",
    )(xn, oa, ob, oc, y5, p, wglu, wmg, wmg, wmg, wmg, wb)


def _out_proj_kernel(mg_ref, w_ref, h_ref, pg_ref, ng_ref, ho_ref, xo_ref):
    y = _dot(mg_ref[...], w_ref[...])
    y = y * lax.rsqrt(jnp.mean(y * y, axis=1, keepdims=True) + NORM_EPS) * pg_ref[...]
    hn = h_ref[...] + y
    ho_ref[...] = hn
    xo_ref[...] = (hn * lax.rsqrt(jnp.mean(hn * hn, axis=1, keepdims=True) + NORM_EPS) * ng_ref[...]).astype(BF16)


def _out_proj(merged, wout, h, post_g, next_g):
    m = h.shape[0]
    tm = 256
    row = lambda i: (i, 0)
    fixed = lambda i: (0, 0)
    return pl.pallas_call(
        _out_proj_kernel,
        grid=(m // tm,),
        in_specs=[pl.BlockSpec((tm, D_MODEL), row),
                  pl.BlockSpec((D_MODEL, D_MODEL), fixed),
                  pl.BlockSpec((tm, D_MODEL), row),
                  pl.BlockSpec((1, D_MODEL), fixed), pl.BlockSpec((1, D_MODEL), fixed)],
        out_specs=[pl.BlockSpec((tm, D_MODEL), row), pl.BlockSpec((tm, D_MODEL), row)],
        out_shape=[jax.ShapeDtypeStruct((m, D_MODEL), F32), jax.ShapeDtypeStruct((m, D_MODEL), BF16)],
        compiler_params=_cparams(("parallel",)),
        name="out_proj",
    )(merged, wout, h, post_g.reshape(1, D_MODEL), next_g.reshape(1, D_MODEL))


def _pad_lanes(v, offset):
    return jnp.zeros((1, LANES), F32).at[0, offset:offset + v.shape[0]].set(v.astype(F32))


def _expand_matrices():
    hh = jnp.arange(SSD_HEADS)
    r = jnp.arange(LANES)[:, None]
    c512 = jnp.arange(WIDTH)[None, :]
    e512 = (r == MISC_DT + c512 // SSD_P).astype(BF16)
    c1024 = jnp.arange(SSD_HEADS * LANES)[None, :]
    e1024 = (r == MISC_DT + c1024 // LANES).astype(BF16)
    del hh
    return e512, e1024


def _rope_tables(seq):
    half = DIFF_QK // 2
    inv = ROPE_THETA ** (-jnp.arange(half, dtype=F32) / half)
    pos = jnp.arange(seq, dtype=F32)
    ang = pos[:, None] * inv[None, :]
    cos = jnp.tile(jnp.cos(ang), (1, LANES // half))
    sin = jnp.sin(ang)
    ssin = jnp.tile(jnp.concatenate([-sin, sin], axis=1), (1, LANES // DIFF_QK))
    return cos, ssin


def _s5_params(a_re, a_im, b_re, b_im, c_re, c_im, log_dt):
    lam = lax.complex(a_re.astype(F32), a_im.astype(F32))
    dt = jnp.exp(log_dt.astype(F32))[:, None]
    a_bar = jnp.exp(lam * dt)
    b_bar = ((a_bar - 1.0) / lam)[..., None] * lax.complex(b_re.astype(F32), b_im.astype(F32))
    eye8 = jnp.eye(8, dtype=F32)

    def in_blockdiag(w):
        w = w.reshape(S5_SLABS, 8, S5_P, S5_GROUP_CH)
        return jnp.einsum('sgph,gk->sghkp', w, eye8).reshape(S5_SLABS, LANES, S5_SLAB_STATE)

    def out_blockdiag(w):
        w = w.reshape(S5_SLABS, 8, S5_GROUP_CH, S5_P)
        return jnp.einsum('sghp,gk->sgpkh', w, eye8).reshape(S5_SLABS, S5_SLAB_STATE, LANES)

    bmat = jnp.concatenate([in_blockdiag(jnp.real(b_bar)), in_blockdiag(jnp.imag(b_bar))], axis=2).astype(BF16)
    cmat = jnp.concatenate([out_blockdiag(c_re.astype(F32)), -out_blockdiag(c_im.astype(F32))], axis=1).astype(BF16)
    are = jnp.real(a_bar).reshape(S5_SLABS, 1, S5_SLAB_STATE)
    aim = jnp.imag(a_bar).reshape(S5_SLABS, 1, S5_SLAB_STATE)
    return bmat, cmat, are, aim


def kernel(x, w_in, b_fox_f, pre_norm_g, post_norm_g, diff_lambda, diff_subln_g, ssd_conv_w, ssd_conv_b, ssd_dt_bias, ssd_a_log, ssd_d, ssd_norm_g, s5_a_re, s5_a_im, s5_b_re, s5_b_im, s5_c_re, s5_c_im, s5_d, s5_log_dt, s5_w_glu, w_branch, w_out):
    batch, seq, d = x.shape
    assert d == D_MODEL and seq % SSD_T == 0 and batch % SUBLANES == 0
    depth = w_in.shape[0]
    m = batch * seq
    W = WIDTH

    o_fq, o_fk, o_fv, o_ff, o_fg = 0, W, 2 * W, 3 * W, 3 * W + N_HEADS
    o_dq = o_fg + W
    o_dk, o_dv, o_dg = o_dq + W, o_dq + 2 * W, o_dq + 3 * W
    o_sz = o_dg + W
    o_xbc = o_sz + W
    o_dt = o_xbc + SSD_CONV_DIM
    o_su = o_dt + SSD_HEADS
    o_sg = o_su + W
    o_mg = o_sg + W
    seg = lambda off, size: w_in[:, :, off:off + size]
    w_proj = jnp.concatenate(
        [seg(o_fq, W), seg(o_fk, W), seg(o_fv, W), seg(o_fg, W),
         seg(o_dq, W), seg(o_dk, W), seg(o_dv, W), seg(o_dg, W),
         seg(o_xbc, SSD_CONV_DIM), seg(o_sz, W), seg(o_su, W), seg(o_sg, W)], axis=2).astype(BF16)
    w_misc = jnp.concatenate(
        [seg(o_ff, N_HEADS), seg(o_dt, SSD_HEADS),
         jnp.zeros((depth, D_MODEL, LANES - N_HEADS - SSD_HEADS), F32)], axis=2).astype(BF16)
    w_mg = seg(o_mg, 4 * D_MODEL).astype(BF16)
    w_branch_b = w_branch.astype(BF16)
    w_out_b = w_out.astype(BF16)
    w_glu_b = s5_w_glu.astype(BF16)

    cos_t, sin_t = _rope_tables(seq)
    e512, e1024 = _expand_matrices()

    h = x.reshape(m, D_MODEL)
    xn = _rmsnorm(h, pre_norm_g[0])
    for l in range(depth):
        lambda_init = 0.8 - 0.6 * math.exp(-0.3 * l)
        p, misc = _in_proj(xn, w_proj[l], w_misc[l], cos_t, sin_t, seq)

        fcol, frow = _fox_prep(misc, _pad_lanes(b_fox_f[l], MISC_FF), batch, seq)
        oa = _fox_attn(p, fcol, frow, batch, seq)
        ob = _diff_attn(p, diff_lambda[l].astype(F32), diff_subln_g[l].astype(F32), lambda_init, batch, seq)
        oc = _ssd(p, misc, ssd_conv_w[l].astype(F32), ssd_conv_b[l].astype(F32).reshape(1, SSD_CONV_DIM),
                  _pad_lanes(ssd_dt_bias[l], MISC_DT), _pad_lanes(ssd_a_log[l], MISC_DT),
                  jnp.repeat(ssd_d[l].astype(F32), SSD_P).reshape(1, W),
                  ssd_norm_g[l].astype(F32).reshape(1, W), e512, e1024, batch, seq)
        bmat, cmat, are, aim = _s5_params(s5_a_re[l], s5_a_im[l], s5_b_re[l], s5_b_im[l],
                                          s5_c_re[l], s5_c_im[l], s5_log_dt[l])
        y5 = _s5(p.reshape(batch, seq, NP_COLS), bmat, cmat, are, aim,
                 s5_d[l].astype(F32).reshape(S5_SLABS, 1, LANES), batch, seq).reshape(m, W)

        merged = _merge(xn, oa, ob, oc, y5, p, w_glu_b[l], w_mg[l], w_branch_b[l], seq)
        next_g = pre_norm_g[l + 1] if l + 1 < depth else pre_norm_g[l]
        h, xn = _out_proj(merged, w_out_b[l], h, post_norm_g[l], next_g)
    return h.reshape(batch, seq, D_MODEL)
```
